```python
import math
import jax
import jax.numpy as jnp
from jax import lax
import numpy as np

D_MODEL = 1024
BATCH = 8
SEQ = 2048
DEPTH = 4
DEC_BATCH = 128
DEC_SEQ = 1
PAST_LEN = 2048
PAGE_SIZE = 128

HD_A = 64
DA = D_MODEL // 2
H_A = DA // HD_A
LORA_W = 64
LORA_A = 64
LORA_G = 128
A_COLS = 3 * DA + LORA_W + LORA_A + LORA_G
A_SPLITS = (DA, 2 * DA, 3 * DA, 3 * DA + LORA_W, 3 * DA + LORA_W + LORA_A)
RWKV_GN_EPS = 64e-5
HD_B = 128
DB = D_MODEL - DA
H_B = DB // HD_B
CONV_W = 4
B_CONV_CH = 3 * DB
B_COLS = 4 * DB + 2 * H_B
GDN_CHUNK = 64
MIX_COLS = A_COLS + B_COLS
D_MIX = DA + DB
H_C = 8
D_C = D_MODEL // (2 * H_C)
ROT_DIM = D_C // 4
ROPE_THETA = 500000.0
Q_BLOCK = 128
D_FF = 2752
N_MOD = 9
NORM_EPS = 1e-6
N_MIX = (DEPTH + 1) // 2
N_ATT = DEPTH // 2
N_PAGES = PAST_LEN // PAGE_SIZE
N_POOL = DEC_BATCH * N_PAGES + max(1, (DEC_BATCH * N_PAGES) // 4)

kernel_name = 'hybrid_rwkv7_gdn_diffattn_macaron_adaln_step'

F32 = jnp.float32


def rms_norm(x, g):
    xf = x.astype(F32)
    y = xf * lax.rsqrt(jnp.mean(xf * xf, axis=-1, keepdims=True) + NORM_EPS)
    return y.astype(x.dtype) * g


def l2_normalize(x, eps=1e-6):
    xf = x.astype(F32)
    return (xf * lax.rsqrt(jnp.sum(xf * xf, axis=-1, keepdims=True) + eps)).astype(x.dtype)


def group_norm_heads(y, w, b):
    yf = y.astype(F32)
    mu = jnp.mean(yf, axis=-1, keepdims=True)
    var = jnp.mean((yf - mu) ** 2, axis=-1, keepdims=True)
    out = ((yf - mu) * lax.rsqrt(var + RWKV_GN_EPS)).astype(y.dtype)
    return out * w.reshape(y.shape[2], y.shape[3]) + b.reshape(y.shape[2], y.shape[3])


def modulate(x, g, shift, scale):
    return rms_norm(x, g) * (1 + scale) + shift


def swiglu(h, w_in, w_out):
    a, b = jnp.split(h @ w_in, 2, axis=-1)
    return (jax.nn.silu(a) * b) @ w_out


def rope_partial(x, pos):
    half = ROT_DIM // 2
    inv = ROPE_THETA ** (-jnp.arange(half, dtype=F32) * 2.0 / ROT_DIM)
    ang = pos.astype(F32)[:, None] * inv[None, :]
    cos = jnp.cos(ang)[:, None, None, :].astype(x.dtype)
    sin = jnp.sin(ang)[:, None, None, :].astype(x.dtype)
    x1, x2, xp = x[..., :half], x[..., half:ROT_DIM], x[..., ROT_DIM:]
    return jnp.concatenate([x1 * cos - x2 * sin, x2 * cos + x1 * sin, xp], axis=-1)


def rwkv7_step(S, inp):
    r, w, k, v, kk, a = inp
    sa = jnp.einsum('bhvk,bhk->bhv', S, -kk)
    S = S * w[:, :, None, :] + sa[..., None] * (kk * a)[:, :, None, :] + v[..., None] * k[:, :, None, :]
    return S, jnp.einsum('bhvk,bhk->bhv', S, r)


def gdn_step(S, inp):
    q, k, v, g, beta = inp
    S = S * jnp.exp(g)[..., None, None]
    err = v - jnp.einsum('bhk,bhkv->bhv', k, S)
    S = S + jnp.einsum('bhk,bhv->bhkv', k * beta[..., None], err)
    return S, jnp.einsum('bhk,bhkv->bhv', q, S)


def gdn_recurrent(q, k, v, g, beta, S0):
    tf = lambda t: jnp.moveaxis(t, 1, 0)
    S, o = lax.scan(gdn_step, S0, (tf(q), tf(k), tf(v), tf(g), tf(beta)))
    return jnp.moveaxis(o, 0, 1), S


def gdn_chunked(q, k, v, g, beta, S0):
    Bn, T, H, N = q.shape
    nc = T // GDN_CHUNK

    def blk(t):
        t = t.reshape((Bn, nc, GDN_CHUNK) + t.shape[2:])
        return jnp.moveaxis(jnp.moveaxis(t, 1, 0), 2, 3)

    qc, kc, vc, gc, bc = blk(q), blk(k), blk(v), blk(g), blk(beta)
    gcum = jnp.cumsum(gc, axis=-1)
    idx = jnp.arange(GDN_CHUNK)
    lower = idx[:, None] >= idx[None, :]
    strict = idx[:, None] > idx[None, :]
    diff = gcum[..., :, None] - gcum[..., None, :]
    decay = jnp.where(lower, jnp.exp(jnp.where(lower, diff, 0.0)), 0.0)
    kb = kc * bc[..., None]
    A = jnp.einsum('nbhik,nbhjk->nbhij', kb, kc) * jnp.where(strict, decay, 0.0)
    rhs = jnp.concatenate([kb * jnp.exp(gcum)[..., None], vc * bc[..., None]], axis=-1)
    sol = lax.linalg.triangular_solve(A.astype(F32), rhs.astype(F32), left_side=True,
                                      lower=True, unit_diagonal=True).astype(q.dtype)
    wc, uc = sol[..., :N], sol[..., N:]
    Aqk = jnp.einsum('nbhik,nbhjk->nbhij', qc, kc) * decay

    def step(S, inp):
        qi, ki, ui, wi, gi, Ai = inp
        v_new = ui - jnp.einsum('bhck,bhkv->bhcv', wi, S)
        o = jnp.einsum('bhck,bhkv->bhcv', qi * jnp.exp(gi)[..., None], S) + jnp.einsum('bhij,bhjv->bhiv', Ai, v_new)
        gl = gi[..., -1:]
        S = S * jnp.exp(gl)[..., None] + jnp.einsum('bhck,bhcv->bhkv', ki * jnp.exp(gl - gi)[..., None], v_new)
        return S, o

    S, o = lax.scan(step, S0, (qc, kc, uc, wc, gcum, Aqk))
    o = jnp.moveaxis(jnp.moveaxis(o, 3, 2), 0, 1).reshape(Bn, T, H, N)
    return o, S


def mixer_ab(h, p, m, shift_prev, wkv0, conv_buf, ssm0, chunked):
    Bn, T, _ = h.shape
    proj = h @ p['mix_w_in'][m]
    proj_a, proj_b = proj[..., :A_COLS], proj[..., A_COLS:]
    prev = jnp.concatenate([shift_prev[:, None, :].astype(proj_a.dtype), proj_a[:, :-1]], axis=1)
    xa = proj_a + p['a_mu'][m] * (prev - proj_a)
    r, k, v, wl, al, gl = jnp.split(xa, A_SPLITS, axis=-1)
    w = -jax.nn.softplus(-(p['a_w0'][m] + jnp.tanh(wl) @ p['a_w2'][m])) - 0.5
    dec = jnp.exp(-jnp.exp(w))
    a = jax.nn.sigmoid(p['a_a0'][m] + al @ p['a_a2'][m])
    gate = jax.nn.sigmoid(gl) @ p['a_g2'][m]
    ha = lambda t: t.reshape(Bn, T, H_A, HD_A)
    kk = l2_normalize(ha(k * p['a_kk'][m]))
    k = k * (1 + (a - 1) * p['a_ka'][m])
    r, k, v, dec, a = ha(r), ha(k), ha(v), ha(dec), ha(a)
    tf = lambda t: jnp.moveaxis(t, 1, 0)
    wkv, ya = lax.scan(rwkv7_step, wkv0, (tf(r), tf(dec), tf(k), tf(v), tf(kk), tf(a)))
    ya = jnp.moveaxis(ya, 0, 1)
    ya = group_norm_heads(ya, p['a_gn_w'][m], p['a_gn_b'][m]) + jnp.sum(r * k * p['a_rk'][m], axis=-1, keepdims=True) * v
    ya = ya.reshape(Bn, T, DA) * gate
    qkv_pre = proj_b[..., :B_CONV_CH]
    z = proj_b[..., B_CONV_CH:4 * DB]
    b_raw = proj_b[..., 4 * DB:4 * DB + H_B]
    a_raw = proj_b[..., 4 * DB + H_B:]
    xp = jnp.concatenate([conv_buf.astype(qkv_pre.dtype), qkv_pre], axis=1)
    cw = p['b_conv_w'][m]
    conv = xp[:, 0:T] * cw[0]
    for i in range(1, CONV_W):
        conv = conv + xp[:, i:i + T] * cw[i]
    new_buf = xp[:, T:]
    qb, kb, vb = jnp.split(jax.nn.silu(conv), 3, axis=-1)
    hb = lambda t: t.reshape(Bn, T, H_B, HD_B)
    qb = l2_normalize(hb(qb)) * (HD_B ** -0.5)
    kb = l2_normalize(hb(kb))
    vb = hb(vb)
    beta = jax.nn.sigmoid(b_raw)
    gdec = -jnp.exp(p['b_a_log'][m]) * jax.nn.softplus(a_raw + p['b_dt_bias'][m])
    if chunked:
        ob, ssm = gdn_chunked(qb, kb, vb, gdec, beta, ssm0)
    else:
        ob, ssm = gdn_recurrent(qb, kb, vb, gdec, beta, ssm0)
    ob = rms_norm(ob, p['b_norm_g'][m]) * jax.nn.silu(hb(z))
    yb = ob.reshape(Bn, T, DB)
    y = jnp.concatenate([ya, yb], axis=-1) @ p['mix_w_out'][m]
    return y, proj_a[:, -1], wkv, new_buf, ssm


def diff_core(q, k, v, q_pos, k_pos, lam):
    s = jnp.einsum('bqhcd,bkhcd->bhcqk', q, k).astype(F32) * (D_C ** -0.5)
    mask = k_pos[None, :] <= q_pos[:, None]
    pr = jax.nn.softmax(jnp.where(mask, s, -jnp.inf), axis=-1)
    pd = (pr[:, :, 0] - lam.astype(F32) * pr[:, :, 1]).astype(v.dtype)
    return jnp.einsum('bhqk,bkhv->bqhv', pd, v)


def mixer_diff(h, p, ai, lam_init, pos, past):
    Bn, T, _ = h.shape
    q, k, v = jnp.split(h @ p['att_w_in'][ai], 3, axis=-1)
    q = rope_partial(q.reshape(Bn, T, H_C, 2, D_C), pos)
    k = rope_partial(k.reshape(Bn, T, H_C, 2, D_C), pos)
    v = v.reshape(Bn, T, H_C, 2 * D_C)
    lam = (jnp.exp(jnp.sum(p['lam_q1'][ai] * p['lam_k1'][ai])) - jnp.exp(jnp.sum(p['lam_q2'][ai] * p['lam_k2'][ai]))
           + lam_init)
    if past is None:
        nb = T // Q_BLOCK
        qb = jnp.moveaxis(q.reshape(Bn, nb, Q_BLOCK, H_C, 2, D_C), 1, 0)
        pb = pos.reshape(nb, Q_BLOCK)
        o = lax.map(lambda qp: diff_core(qp[0], k, v, qp[1], pos, lam), (qb, pb))
        o = jnp.moveaxis(o, 0, 1).reshape(Bn, T, H_C, 2 * D_C)
    else:
        past_k, past_v = past
        k_all = jnp.concatenate([past_k.reshape(Bn, -1, H_C, 2, D_C).astype(k.dtype), k], axis=1)
        v_all = jnp.concatenate([past_v.astype(v.dtype), v], axis=1)
        o = diff_core(q, k_all, v_all, pos, jnp.arange(k_all.shape[1]), lam)
    o = rms_norm(o, p['att_norm_g'][ai]) * (1 - lam_init)
    y = o.reshape(Bn, T, D_MODEL) @ p['att_w_out'][ai]
    return y, k.reshape(Bn, T, H_C, 2 * D_C), v


def run_trunk(x, c, pos, p, chunked, shift0, wkv0, conv0, ssm0, cache_k, cache_v, page_table):
    Bn = x.shape[0]
    shifts, wkvs, convs, ssms, ks, vs = [], [], [], [], [], []
    for l in range(DEPTH):
        mod = (jax.nn.silu(c) @ p['ada_w'][l] + p['ada_b'][l]).reshape(Bn, N_MOD, 1, D_MODEL)
        h = modulate(x, p['norm_g'][l, 0], mod[:, 0], mod[:, 1])
        x = x + 0.5 * mod[:, 2] * swiglu(h, p['ffn_w_in'][l, 0], p['ffn_w_out'][l, 0])
        h = modulate(x, p['norm_g'][l, 1], mod[:, 3], mod[:, 4])
        if l % 2 == 0:
            m = l // 2
            y, s_sh, s_wkv, s_conv, s_ssm = mixer_ab(h, p, m, shift0[m], wkv0[m], conv0[m], ssm0[m], chunked)
            shifts.append(s_sh)
            wkvs.append(s_wkv)
            convs.append(s_conv)
            ssms.append(s_ssm)
        else:
            ai = l // 2
            if cache_k is None:
                past = None
            else:
                past = (cache_k[ai][page_table].reshape(Bn, PAST_LEN, H_C, 2 * D_C),
                        cache_v[ai][page_table].reshape(Bn, PAST_LEN, H_C, 2 * D_C))
            lam_init = 0.8 - 0.6 * math.exp(-0.3 * l)
            y, k_new, v_new = mixer_diff(h, p, ai, lam_init, pos, past)
            ks.append(k_new)
            vs.append(v_new)
        x = x + mod[:, 5] * y
        h = modulate(x, p['norm_g'][l, 2], mod[:, 6], mod[:, 7])
        x = x + 0.5 * mod[:, 8] * swiglu(h, p['ffn_w_in'][l, 1], p['ffn_w_out'][l, 1])
    return (rms_norm(x, p['final_g']), jnp.stack(shifts), jnp.stack(wkvs), jnp.stack(convs),
            jnp.stack(ssms), jnp.stack(ks), jnp.stack(vs))


def setup_inputs(seed: int = 0) -> dict:
    key = jax.random.key(seed)
    keys = list(jax.random.split(key, 64))

    def nrm(shape, scale):
        return jax.random.normal(keys.pop(), shape, F32) * scale

    def unif(shape, lo, hi):
        return jax.random.uniform(keys.pop(), shape, F32, lo, hi)

    page_table = jax.random.permutation(keys.pop(), N_POOL)[:DEC_BATCH * N_PAGES]
    page_table = page_table.reshape(DEC_BATCH, N_PAGES).astype(jnp.int32)
    return {
        'x_prompt': nrm((BATCH, SEQ, D_MODEL), 1.0),
        'x_sample': nrm((DEC_BATCH, DEC_SEQ, D_MODEL), 1.0),
        'state_a_shift': nrm((N_MIX, DEC_BATCH, A_COLS), 1.0),
        'state_a_wkv': nrm((N_MIX, DEC_BATCH, H_A, HD_A, HD_A), 0.2),
        'state_b_conv': nrm((N_MIX, DEC_BATCH, CONV_W - 1, B_CONV_CH), 1.0),
        'state_b_ssm': nrm((N_MIX, DEC_BATCH, H_B, HD_B, HD_B), 0.1),
        'cache_k': nrm((N_ATT, N_POOL, PAGE_SIZE, H_C, 2 * D_C), 1.0),
        'cache_v': nrm((N_ATT, N_POOL, PAGE_SIZE, H_C, 2 * D_C), 1.0),
        'page_table': page_table,
        'c_prompt': nrm((BATCH, D_MODEL), 1.0),
        'c_sample': nrm((DEC_BATCH, D_MODEL), 1.0),
        'ada_w': nrm((DEPTH, D_MODEL, N_MOD * D_MODEL), 0.5 * D_MODEL ** -0.5),
        'ada_b': nrm((DEPTH, N_MOD * D_MODEL), 0.01),
        'norm_g': 1.0 + nrm((DEPTH, 3, D_MODEL), 0.02),
        'ffn_w_in': nrm((DEPTH, 2, D_MODEL, 2 * D_FF), D_MODEL ** -0.5),
        'ffn_w_out': nrm((DEPTH, 2, D_FF, D_MODEL), D_FF ** -0.5),
        'mix_w_in': nrm((N_MIX, D_MODEL, MIX_COLS), D_MODEL ** -0.5),
        'mix_w_out': nrm((N_MIX, D_MIX, D_MODEL), D_MIX ** -0.5),
        'a_mu': unif((N_MIX, A_COLS), 0.0, 1.0),
        'a_w0': unif((N_MIX, DA), -5.0, 1.0),
        'a_w2': nrm((N_MIX, LORA_W, DA), 0.5 * LORA_W ** -0.5),
        'a_a0': nrm((N_MIX, DA), 0.1),
        'a_a2': nrm((N_MIX, LORA_A, DA), LORA_A ** -0.5),
        'a_g2': nrm((N_MIX, LORA_G, DA), LORA_G ** -0.5),
        'a_kk': 0.85 + nrm((N_MIX, DA), 0.02),
        'a_ka': 1.0 + nrm((N_MIX, DA), 0.02),
        'a_rk': nrm((N_MIX, H_A, HD_A), 0.1),
        'a_gn_w': 1.0 + nrm((N_MIX, DA), 0.02),
        'a_gn_b': nrm((N_MIX, DA), 0.01),
        'b_conv_w': nrm((N_MIX, CONV_W, B_CONV_CH), CONV_W ** -0.5),
        'b_a_log': jnp.log(unif((N_MIX, H_B), 1.0, 16.0)),
        'b_dt_bias': unif((N_MIX, H_B), -4.5, -2.0),
        'b_norm_g': 1.0 + nrm((N_MIX, HD_B), 0.02),
        'att_w_in': nrm((N_ATT, D_MODEL, 3 * D_MODEL), D_MODEL ** -0.5),
        'att_w_out': nrm((N_ATT, D_MODEL, D_MODEL), D_MODEL ** -0.5),
        'lam_q1': nrm((N_ATT, D_C), 0.1),
        'lam_k1': nrm((N_ATT, D_C), 0.1),
        'lam_q2': nrm((N_ATT, D_C), 0.1),
        'lam_k2': nrm((N_ATT, D_C), 0.1),
        'att_norm_g': 1.0 + nrm((N_ATT, 2 * D_C), 0.02),
        'final_g': 1.0 + nrm((D_MODEL,), 0.02),
    }


def reference(x_prompt, x_sample, state_a_shift, state_a_wkv, state_b_conv, state_b_ssm, cache_k, cache_v,
              page_table, c_prompt, c_sample, ada_w, ada_b, norm_g, ffn_w_in, ffn_w_out, mix_w_in, mix_w_out,
              a_mu, a_w0, a_w2, a_a0, a_a2, a_g2, a_kk, a_ka, a_rk, a_gn_w, a_gn_b, b_conv_w, b_a_log, b_dt_bias,
              b_norm_g, att_w_in, att_w_out, lam_q1, lam_k1, lam_q2, lam_k2, att_norm_g, final_g):
    p = dict(ada_w=ada_w, ada_b=ada_b, norm_g=norm_g, ffn_w_in=ffn_w_in, ffn_w_out=ffn_w_out,
             mix_w_in=mix_w_in, mix_w_out=mix_w_out, a_mu=a_mu, a_w0=a_w0, a_w2=a_w2, a_a0=a_a0, a_a2=a_a2,
             a_g2=a_g2, a_kk=a_kk, a_ka=a_ka, a_rk=a_rk, a_gn_w=a_gn_w, a_gn_b=a_gn_b, b_conv_w=b_conv_w,
             b_a_log=b_a_log, b_dt_bias=b_dt_bias, b_norm_g=b_norm_g, att_w_in=att_w_in, att_w_out=att_w_out,
             lam_q1=lam_q1, lam_k1=lam_k1, lam_q2=lam_q2, lam_k2=lam_k2, att_norm_g=att_norm_g, final_g=final_g)
    bp, tp = x_prompt.shape[0], x_prompt.shape[1]
    dt = x_prompt.dtype
    (y_prompt, p_a_shift, p_a_wkv, p_b_conv, p_b_ssm, p_k, p_v) = run_trunk(
        x_prompt, c_prompt, jnp.arange(tp), p, True,
        jnp.zeros((N_MIX, bp, A_COLS), dt), jnp.zeros((N_MIX, bp, H_A, HD_A, HD_A), dt),
        jnp.zeros((N_MIX, bp, CONV_W - 1, B_CONV_CH), dt), jnp.zeros((N_MIX, bp, H_B, HD_B, HD_B), dt),
        None, None, None)
    (y_sample, s_a_shift, s_a_wkv, s_b_conv, s_b_ssm, s_k, s_v) = run_trunk(
        x_sample, c_sample, PAST_LEN + jnp.arange(x_sample.shape[1]), p, False,
        state_a_shift, state_a_wkv, state_b_conv, state_b_ssm, cache_k, cache_v, page_table)
    return (y_prompt, y_sample, p_a_shift, p_a_wkv, p_b_conv, p_b_ssm, p_k, p_v,
            s_a_shift, s_a_wkv, s_b_conv, s_b_ssm, s_k, s_v)
```

```python
import functools
import math

import jax
import jax.numpy as jnp
from jax import lax
from jax.experimental import pallas as pl
from jax.experimental.pallas import tpu as pltpu

F32 = jnp.float32
BF16 = jnp.bfloat16
HI = lax.Precision.HIGHEST

D_MODEL = 1024
DEPTH = 4
PAGE_SIZE = 128
HD_A = 64
DA = 512
H_A = 8
LORA_W = 64
LORA_A = 64
LORA_G = 128
A_COLS = 3 * DA + LORA_W + LORA_A + LORA_G
LORA_COLS = LORA_W + LORA_A + LORA_G
RWKV_GN_EPS = 64e-5
HD_B = 128
DB = 512
H_B = 4
CONV_W = 4
B_CONV_CH = 3 * DB
B_COLS = 4 * DB + 2 * H_B
B_COLS_PAD = 4 * DB + 128
H_C = 8
D_C = 64
ROT_DIM = 16
ROPE_THETA = 500000.0
D_FF = 2752
D_FF_PAD = 2816
N_MOD = 9
NORM_EPS = 1e-6
L2_EPS = 1e-6
LANES = 128
SUBLANES = 8
VMEM_LIMIT = 56 * 1024 * 1024


def _cparams(*sem):
    return pltpu.CompilerParams(dimension_semantics=sem, vmem_limit_bytes=VMEM_LIMIT)


def _silu(x):
    return x * jax.nn.sigmoid(x)


def _softplus(z):
    return jnp.maximum(z, 0.0) + jnp.log1p(jnp.exp(-jnp.abs(z)))


def _modulate(x, g, shift, scale):
    y = x * lax.rsqrt(jnp.mean(x * x, axis=-1, keepdims=True) + NORM_EPS)
    return (y * g) * (1.0 + scale) + shift


def _mm(x, y):
    return jnp.dot(x.astype(BF16), y.astype(BF16), preferred_element_type=F32)


def _mm_nt(x, y):
    return lax.dot_general(x.astype(BF16), y.astype(BF16), (((1,), (1,)), ((), ())), preferred_element_type=F32)


def _mm_tn(x, y):
    return lax.dot_general(x.astype(BF16), y.astype(BF16), (((0,), (0,)), ((), ())), preferred_element_type=F32)


def _mm_hi(x, y):
    return jnp.dot(x, y, precision=HI, preferred_element_type=F32)


class _Group:
    def __init__(self, n_seq, seq_len, mod):
        self.n_seq = n_seq
        self.seq_len = seq_len
        self.rows = n_seq * seq_len
        self.mod = mod.reshape(n_seq, 1, -1) if seq_len > 1 else mod.reshape(1, n_seq, -1)

    def row_block(self, want):
        return min(want, self.seq_len) if self.seq_len > 1 else self.rows

    def mod_spec(self, k, tm):
        if self.seq_len > 1:
            bps = self.seq_len // tm
            return pl.BlockSpec((1, 1, D_MODEL), lambda i, *_: (i // bps, 0, k))
        return pl.BlockSpec((1, tm, D_MODEL), lambda i, *_: (0, i, k))


def _ada_kernel(c_ref, w_ref, b_ref, o_ref):
    o_ref[0] = _mm(_silu(c_ref[...]), w_ref[0]) + b_ref[0]


def _ada_mod(c_all, ada_w, ada_b):
    rows = c_all.shape[0]
    n = ada_w.shape[2]
    tn = 1024
    return pl.pallas_call(
        _ada_kernel,
        grid=(DEPTH, n // tn),
        in_specs=[pl.BlockSpec((rows, D_MODEL), lambda l, j: (0, 0)),
                  pl.BlockSpec((1, D_MODEL, tn), lambda l, j: (l, 0, j)),
                  pl.BlockSpec((1, 1, tn), lambda l, j: (l, 0, j))],
        out_specs=pl.BlockSpec((1, rows, tn), lambda l, j: (l, 0, j)),
        out_shape=jax.ShapeDtypeStruct((DEPTH, rows, n), F32),
        compiler_params=_cparams("parallel", "parallel"),
        name="ada_mod",
    )(c_all, ada_w, ada_b.reshape(DEPTH, 1, n))


def _nm_kernel(x_ref, g_ref, sh_ref, sc_ref, w_ref, *rest, rope_tiles, tn):
    if rope_tiles:
        cos_ref, sin_up_ref, sin_dn_ref, o_ref, h_ref = rest
    else:
        o_ref, h_ref = rest
    j = pl.program_id(1)

    @pl.when(j == 0)
    def _():
        h_ref[...] = _modulate(x_ref[...], g_ref[...], sh_ref[0], sc_ref[0]).astype(BF16)

    y = jnp.dot(h_ref[...], w_ref[...], preferred_element_type=F32)
    if rope_tiles:
        half = ROT_DIM // 2

        @pl.when(j < rope_tiles)
        def _():
            o_ref[...] = (y * cos_ref[...] + pltpu.roll(y, half, 1) * sin_up_ref[...]
                          + pltpu.roll(y, tn - half, 1) * sin_dn_ref[...])

        @pl.when(j >= rope_tiles)
        def _():
            o_ref[...] = y
    else:
        o_ref[...] = y


def _norm_mod_matmul(grp, x, g, k_shift, k_scale, w, tm, tn, rope=None):
    m, n = grp.rows, w.shape[1]
    tm = grp.row_block(tm)
    tn = min(tn, n)
    in_specs = [pl.BlockSpec((tm, D_MODEL), lambda i, j: (i, 0)),
                pl.BlockSpec((1, D_MODEL), lambda i, j: (0, 0)),
                grp.mod_spec(k_shift, tm), grp.mod_spec(k_scale, tm),
                pl.BlockSpec((D_MODEL, tn), lambda i, j: (0, j))]
    args = [x, g.reshape(1, D_MODEL), grp.mod, grp.mod, w]
    rope_tiles = 0
    if rope is not None:
        rope_tiles = 2 * D_MODEL // tn
        if grp.seq_len > 1:
            bps = grp.seq_len // tm
            tspec = pl.BlockSpec((tm, tn), lambda i, j: (i % bps, 0))
        else:
            tspec = pl.BlockSpec((1, tn), lambda i, j: (0, 0))
        in_specs += [tspec, tspec, tspec]
        args += [jnp.tile(t, (1, tn // LANES)) for t in rope]
    return pl.pallas_call(
        functools.partial(_nm_kernel, rope_tiles=rope_tiles, tn=tn),
        grid=(m // tm, n // tn),
        in_specs=in_specs,
        out_specs=pl.BlockSpec((tm, tn), lambda i, j: (i, j)),
        out_shape=jax.ShapeDtypeStruct((m, n), F32),
        scratch_shapes=[pltpu.VMEM((tm, D_MODEL), BF16)],
        compiler_params=_cparams("parallel", "arbitrary"),
        name="norm_mod_matmul",
    )(*args)


def _ffn_kernel(x_ref, g_ref, sh_ref, sc_ref, gt_ref, wa_ref, wb_ref, wo_ref, o_ref, h_ref, acc_ref):
    j = pl.program_id(1)

    @pl.when(j == 0)
    def _():
        h_ref[...] = _modulate(x_ref[...], g_ref[...], sh_ref[0], sc_ref[0]).astype(BF16)
        acc_ref[...] = jnp.zeros_like(acc_ref)

    h = h_ref[...]
    a = jnp.dot(h, wa_ref[...], preferred_element_type=F32)
    b = jnp.dot(h, wb_ref[...], preferred_element_type=F32)
    acc_ref[...] += jnp.dot((_silu(a) * b).astype(BF16), wo_ref[...], preferred_element_type=F32)

    @pl.when(j == pl.num_programs(1) - 1)
    def _():
        o_ref[...] = x_ref[...] + (0.5 * gt_ref[0]) * acc_ref[...]


def _ffn(grp, x, g, k0, wa, wb, wo, tm, tf):
    m = grp.rows
    tm = grp.row_block(tm)
    row = pl.BlockSpec((tm, D_MODEL), lambda i, j: (i, 0))
    return pl.pallas_call(
        _ffn_kernel,
        grid=(m // tm, D_FF_PAD // tf),
        in_specs=[row, pl.BlockSpec((1, D_MODEL), lambda i, j: (0, 0)),
                  grp.mod_spec(k0, tm), grp.mod_spec(k0 + 1, tm), grp.mod_spec(k0 + 2, tm),
                  pl.BlockSpec((D_MODEL, tf), lambda i, j: (0, j)),
                  pl.BlockSpec((D_MODEL, tf), lambda i, j: (0, j)),
                  pl.BlockSpec((tf, D_MODEL), lambda i, j: (j, 0))],
        out_specs=row,
        out_shape=jax.ShapeDtypeStruct((m, D_MODEL), F32),
        scratch_shapes=[pltpu.VMEM((tm, D_MODEL), BF16), pltpu.VMEM((tm, D_MODEL), F32)],
        compiler_params=_cparams("parallel", "arbitrary"),
        name="ffn",
    )(x, g.reshape(1, D_MODEL), grp.mod, grp.mod, grp.mod, wa, wb, wo)


def _outproj_kernel(*refs, n_in):
    ys, ws = refs[:n_in], refs[n_in:2 * n_in]
    x_ref, gt_ref, o_ref = refs[2 * n_in:]
    acc = _mm(ys[0][...], ws[0][...])
    for y_ref, w_ref in zip(ys[1:], ws[1:]):
        acc += _mm(y_ref[...], w_ref[...])
    o_ref[...] = x_ref[...] + gt_ref[0] * acc


def _out_proj(grp, ys, ws, x, k_gate, tm):
    m = grp.rows
    tm = grp.row_block(tm)
    row = pl.BlockSpec((tm, D_MODEL), lambda i: (i, 0))
    return pl.pallas_call(
        functools.partial(_outproj_kernel, n_in=len(ys)),
        grid=(m // tm,),
        in_specs=([pl.BlockSpec((tm, y.shape[1]), lambda i: (i, 0)) for y in ys]
                  + [pl.BlockSpec(w.shape, lambda i: (0, 0)) for w in ws]
                  + [row, grp.mod_spec(k_gate, tm)]),
        out_specs=row,
        out_shape=jax.ShapeDtypeStruct((m, D_MODEL), F32),
        compiler_params=_cparams("parallel"),
        name="out_proj",
    )(*ys, *ws, x, grp.mod)


def _final_norm_kernel(x_ref, g_ref, o_ref):
    x = x_ref[...]
    o_ref[...] = (x * lax.rsqrt(jnp.mean(x * x, axis=-1, keepdims=True) + NORM_EPS)) * g_ref[...]


def _final_norm(grp, x, g, tm):
    tm = grp.row_block(tm)
    row = pl.BlockSpec((tm, D_MODEL), lambda i: (i, 0))
    return pl.pallas_call(
        _final_norm_kernel, grid=(grp.rows // tm,),
        in_specs=[row, pl.BlockSpec((1, D_MODEL), lambda i: (0, 0))], out_specs=row,
        out_shape=jax.ShapeDtypeStruct((grp.rows, D_MODEL), F32),
        compiler_params=_cparams("parallel"), name="final_norm",
    )(x, g.reshape(1, D_MODEL))


def _halo(x3, carry, tm, depth):
    n_seq, t, c = x3.shape
    nb = t // tm
    tails = x3.reshape(n_seq, nb, tm, c)[:, :nb - 1, tm - depth:, :]
    prev = jnp.concatenate([carry[:, None].astype(x3.dtype), tails], axis=1)
    prev = jnp.pad(prev, ((0, 0), (0, 0), (SUBLANES - depth, 0), (0, 0)))
    return prev.reshape(n_seq * nb, SUBLANES, c)


def _rwkv_prep_math(pa, prev, mu, w0, w2p, a0, a2p, g2p, kkp, ka, rk, gsum):
    xa = pa + mu * (prev - pa)
    r, k, v = xa[:, :DA], xa[:, DA:2 * DA], xa[:, 2 * DA:3 * DA]
    tail = xa[:, 3 * DA:]
    w = -_softplus(-(w0 + _mm(jnp.tanh(tail), w2p))) - 0.5
    lw = -jnp.exp(w)
    a = jax.nn.sigmoid(a0 + _mm(tail, a2p))
    gate = _mm(jax.nn.sigmoid(tail), g2p)
    kx = k * kkp
    kk = kx * lax.rsqrt(_mm_hi(kx * kx, gsum) + L2_EPS)
    k2 = k * (1.0 + (a - 1.0) * ka)
    bonus = _mm_hi(r * k2 * rk, gsum) * v
    return r, lw, k2, v, -kk, kk * a, bonus, gate


def _rwkv_prep_seq_kernel(pa_ref, halo_ref, *refs, tm):
    params, outs, xs_ref = refs[:10], refs[10:18], refs[18]
    xs_ref[0:SUBLANES, :] = halo_ref[0]
    xs_ref[SUBLANES:, :] = pa_ref[...]
    res = _rwkv_prep_math(pa_ref[...], xs_ref[SUBLANES - 1:SUBLANES - 1 + tm, :], *[p[...] for p in params])
    for o_ref, val in zip(outs, res):
        o_ref[...] = val


def _rwkv_prep_row_kernel(pa_ref, prev_ref, *refs):
    params, outs = refs[:10], refs[10:18]
    res = _rwkv_prep_math(pa_ref[...], prev_ref[...], *[p[...] for p in params])
    for o_ref, val in zip(outs, res):
        o_ref[...] = val


def _rwkv_prep(grp, proj_a, shift_prev, params, tm):
    m = grp.rows
    tm = grp.row_block(tm)
    row_a = pl.BlockSpec((tm, A_COLS), lambda i: (i, 0))
    row_o = pl.BlockSpec((tm, DA), lambda i: (i, 0))
    pspecs = [pl.BlockSpec(p.shape, lambda i: (0, 0)) for p in params]
    out_shape = [jax.ShapeDtypeStruct((m, DA), F32)] * 8
    if grp.seq_len > 1:
        halo = _halo(proj_a.reshape(grp.n_seq, grp.seq_len, A_COLS), shift_prev[:, None, :], tm, 1)
        return pl.pallas_call(
            functools.partial(_rwkv_prep_seq_kernel, tm=tm), grid=(m // tm,),
            in_specs=[row_a, pl.BlockSpec((1, SUBLANES, A_COLS), lambda i: (i, 0, 0))] + pspecs,
            out_specs=[row_o] * 8, out_shape=out_shape,
            scratch_shapes=[pltpu.VMEM((tm + SUBLANES, A_COLS), F32)],
            compiler_params=_cparams("parallel"), name="rwkv_prep",
        )(proj_a, halo, *params)
    return pl.pallas_call(
        _rwkv_prep_row_kernel, grid=(m // tm,),
        in_specs=[row_a, row_a] + pspecs, out_specs=[row_o] * 8, out_shape=out_shape,
        compiler_params=_cparams("parallel"), name="rwkv_prep",
    )(proj_a, shift_prev, *params)


def _rwkv_post_kernel(y_ref, bonus_ref, gate_ref, w_ref, b_ref, gsum_ref, o_ref):
    y = y_ref[...]
    gsum = gsum_ref[...]
    d = y - _mm_hi(y, gsum) * (1.0 / HD_A)
    var = _mm_hi(d * d, gsum) * (1.0 / HD_A)
    yn = d * lax.rsqrt(var + RWKV_GN_EPS)
    o_ref[...] = (yn * w_ref[...] + b_ref[...] + bonus_ref[...]) * gate_ref[...]


def _rwkv_post(grp, y, bonus, gate, gn_w, gn_b, gsum, tm):
    tm = grp.row_block(tm)
    row = pl.BlockSpec((tm, DA), lambda i: (i, 0))
    vec = pl.BlockSpec((1, DA), lambda i: (0, 0))
    return pl.pallas_call(
        _rwkv_post_kernel, grid=(grp.rows // tm,),
        in_specs=[row, row, row, vec, vec, pl.BlockSpec((DA, DA), lambda i: (0, 0))], out_specs=row,
        out_shape=jax.ShapeDtypeStruct((grp.rows, DA), F32),
        compiler_params=_cparams("parallel"), name="rwkv_post",
    )(y, bonus, gate, gn_w.reshape(1, DA), gn_b.reshape(1, DA), gsum)


def _neumann_inverse(x, n):
    eye = (lax.broadcasted_iota(jnp.int32, (n, n), 0) == lax.broadcasted_iota(jnp.int32, (n, n), 1)).astype(F32)
    t = eye + x
    for _ in range(int(math.log2(n)) - 1):
        x = _mm_hi(x, x)
        t = t + _mm_hi(t, x)
    return t


def _rwkv_chunk_kernel(r_ref, lw_ref, k_ref, v_ref, a_ref, b_ref, s0_ref, y_ref, st_ref, *, C, nb):
    c = pl.program_id(2)

    @pl.when(c == 0)
    def _():
        st_ref[...] = s0_ref[...]

    row = lax.broadcasted_iota(jnp.int32, (C, C), 0)
    col = lax.broadcasted_iota(jnp.int32, (C, C), 1)
    lower = row >= col
    strict = row > col
    tri = lower.astype(F32)
    head0 = lax.broadcasted_iota(jnp.int32, (C, LANES), 1) < HD_A
    same_head = ((lax.broadcasted_iota(jnp.int32, (LANES, LANES), 0) < HD_A)
                 == (lax.broadcasted_iota(jnp.int32, (LANES, LANES), 1) < HD_A))
    for n in range(nb):
        r, lw, k, v, a, b = r_ref[n], lw_ref[n], k_ref[n], v_ref[n], a_ref[n], b_ref[n]
        s = st_ref[n, 0]
        cum = _mm_hi(tri, lw)
        tot = cum[C - 1:C, :]
        mid = 0.5 * tot
        e_mid = jnp.exp(mid)
        at = a * jnp.exp(cum - lw - mid)
        rt = r * jnp.exp(cum - mid)
        e_key = jnp.exp(mid - cum)
        bt = b * e_key
        kt = k * e_key
        us, ys = [], []
        for hd in range(2):
            mine = head0 if hd == 0 else jnp.logical_not(head0)
            ah = jnp.where(mine, at, 0.0)
            rh = jnp.where(mine, rt, 0.0)
            a_ab = jnp.where(strict, _mm_nt(ah, bt), 0.0)
            a_ak = jnp.where(strict, _mm_nt(ah, kt), 0.0)
            r_b = jnp.where(lower, _mm_nt(rh, bt), 0.0)
            r_k = jnp.where(lower, _mm_nt(rh, kt), 0.0)
            u = _mm(_neumann_inverse(a_ab, C), _mm_nt(ah * e_mid, s) + _mm(a_ak, v))
            us.append(u)
            ys.append(_mm_nt(rh * e_mid, s) + _mm(r_b, u) + _mm(r_k, v))
        u = jnp.where(head0, us[0], us[1])
        y_ref[n] = jnp.where(head0, ys[0], ys[1])
        e_end = jnp.exp(tot - mid)
        upd = _mm_tn(u, bt * e_end) + _mm_tn(v, kt * e_end)
        st_ref[n, 0] = s * jnp.exp(tot) + jnp.where(same_head, upd, 0.0)


def _rwkv_scan(vecs, s0, C, nb):
    bn, t, _ = vecs[0].shape
    vspec = pl.BlockSpec((nb, C, LANES), lambda i, p, c: (i, c, p))
    sspec = pl.BlockSpec((nb, 1, LANES, LANES), lambda i, p, c: (i, p, 0, 0))
    return pl.pallas_call(
        functools.partial(_rwkv_chunk_kernel, C=C, nb=nb),
        grid=(bn // nb, H_A // 2, t // C),
        in_specs=[vspec] * 6 + [sspec],
        out_specs=[vspec, sspec],
        out_shape=[jax.ShapeDtypeStruct((bn, t, DA), F32), jax.ShapeDtypeStruct(s0.shape, F32)],
        compiler_params=_cparams("parallel", "parallel", "arbitrary"),
        name="rwkv_scan",
    )(*vecs, s0)


def _pair_diag(s):
    bn, h, n, _ = s.shape
    s = s.reshape(bn, h // 2, 2, n, n)
    z = jnp.zeros_like(s[:, :, 0])
    top = jnp.concatenate([s[:, :, 0], z], axis=-1)
    bot = jnp.concatenate([z, s[:, :, 1]], axis=-1)
    return jnp.concatenate([top, bot], axis=-2)


def _unpair_diag(sd):
    bn, hp, n2, _ = sd.shape
    n = n2 // 2
    return jnp.stack([sd[:, :, :n, :n], sd[:, :, n:, n:]], axis=2).reshape(bn, hp * 2, n, n)


def _head_l2norm(x, n_heads, width):
    parts = []
    for h in range(n_heads):
        xh = x[:, h * width:(h + 1) * width]
        parts.append(xh * lax.rsqrt(jnp.sum(xh * xh, axis=-1, keepdims=True) + L2_EPS))
    return jnp.concatenate(parts, axis=-1)


def _gdn_prep_math(taps, ba, cw, alog, dtb):
    conv = taps[0] * cw[0:1, :]
    for i in range(1, CONV_W):
        conv = conv + taps[i] * cw[i:i + 1, :]
    act = _silu(conv)
    q = _head_l2norm(act[:, :DB], H_B, HD_B) * (HD_B ** -0.5)
    k = _head_l2norm(act[:, DB:2 * DB], H_B, HD_B)
    v = act[:, 2 * DB:]
    lane = lax.broadcasted_iota(jnp.int32, ba.shape, 1)
    gates = jnp.where(lane < H_B, jax.nn.sigmoid(ba), -jnp.exp(alog) * _softplus(ba + dtb))
    return q, k, v, gates


def _gdn_prep_seq_kernel(x_ref, halo_ref, ba_ref, cw_ref, alog_ref, dtb_ref, q_ref, k_ref, v_ref, gt_ref, xs_ref,
                         *, tm):
    xs_ref[0:SUBLANES, :] = halo_ref[0]
    xs_ref[SUBLANES:, :] = x_ref[...]
    first = SUBLANES - (CONV_W - 1)
    taps = [xs_ref[first + i:first + i + tm, :] for i in range(CONV_W)]
    res = _gdn_prep_math(taps, ba_ref[...], cw_ref[...], alog_ref[...], dtb_ref[...])
    for o_ref, val in zip((q_ref, k_ref, v_ref, gt_ref), res):
        o_ref[...] = val


def _gdn_prep_row_kernel(x_ref, buf_ref, ba_ref, cw_ref, alog_ref, dtb_ref, q_ref, k_ref, v_ref, gt_ref):
    taps = [buf_ref[i] for i in range(CONV_W - 1)] + [x_ref[...]]
    res = _gdn_prep_math(taps, ba_ref[...], cw_ref[...], alog_ref[...], dtb_ref[...])
    for o_ref, val in zip((q_ref, k_ref, v_ref, gt_ref), res):
        o_ref[...] = val


def _gdn_prep(grp, proj_b, conv_buf, cw, alog, dtb, tm):
    m = grp.rows
    tm = grp.row_block(tm)
    xspec = pl.BlockSpec((tm, B_CONV_CH), lambda i: (i, 0))
    baspec = pl.BlockSpec((tm, LANES), lambda i: (i, 4 * DB // LANES))
    pspecs = [pl.BlockSpec((CONV_W, B_CONV_CH), lambda i: (0, 0)),
              pl.BlockSpec((1, LANES), lambda i: (0, 0)), pl.BlockSpec((1, LANES), lambda i: (0, 0))]
    row = pl.BlockSpec((tm, DB), lambda i: (i, 0))
    out_specs = [row, row, row, pl.BlockSpec((tm, LANES), lambda i: (i, 0))]
    out_shape = [jax.ShapeDtypeStruct((m, DB), F32)] * 3 + [jax.ShapeDtypeStruct((m, LANES), F32)]
    if grp.seq_len > 1:
        x3 = proj_b.reshape(grp.n_seq, grp.seq_len, B_COLS_PAD)[:, :, :B_CONV_CH]
        halo = _halo(x3, conv_buf, tm, CONV_W - 1)
        return pl.pallas_call(
            functools.partial(_gdn_prep_seq_kernel, tm=tm), grid=(m // tm,),
            in_specs=[xspec, pl.BlockSpec((1, SUBLANES, B_CONV_CH), lambda i: (i, 0, 0)), baspec] + pspecs,
            out_specs=out_specs, out_shape=out_shape,
            scratch_shapes=[pltpu.VMEM((tm + SUBLANES, B_CONV_CH), F32)],
            compiler_params=_cparams("parallel"), name="gdn_prep",
        )(proj_b, halo, proj_b, cw, alog, dtb)
    buf = jnp.transpose(conv_buf, (1, 0, 2))
    return pl.pallas_call(
        _gdn_prep_row_kernel, grid=(m // tm,),
        in_specs=[xspec, pl.BlockSpec((CONV_W - 1, tm, B_CONV_CH), lambda i: (0, i, 0)), baspec] + pspecs,
        out_specs=out_specs, out_shape=out_shape,
        compiler_params=_cparams("parallel"), name="gdn_prep",
    )(proj_b, buf, proj_b, cw, alog, dtb)


def _gdn_post_kernel(o_ref, z_ref, g_ref, out_ref):
    o = o_ref[...]
    parts = []
    for h in range(H_B):
        oh = o[:, h * HD_B:(h + 1) * HD_B]
        parts.append(oh * lax.rsqrt(jnp.mean(oh * oh, axis=-1, keepdims=True) + NORM_EPS) * g_ref[...])
    out_ref[...] = jnp.concatenate(parts, axis=-1) * _silu(z_ref[...])


def _gdn_post(grp, o, proj_b, norm_g, tm):
    tm = grp.row_block(tm)
    row = pl.BlockSpec((tm, DB), lambda i: (i, 0))
    return pl.pallas_call(
        _gdn_post_kernel, grid=(grp.rows // tm,),
        in_specs=[row, pl.BlockSpec((tm, DB), lambda i: (i, B_CONV_CH // DB)),
                  pl.BlockSpec((1, HD_B), lambda i: (0, 0))],
        out_specs=row, out_shape=jax.ShapeDtypeStruct((grp.rows, DB), F32),
        compiler_params=_cparams("parallel"), name="gdn_post",
    )(o, proj_b, norm_g.reshape(1, HD_B))


def _gdn_chunk_kernel(q_ref, k_ref, v_ref, gc_ref, gr_ref, bc_ref, s0_ref, o_ref, st_ref, *, C, nb):
    c = pl.program_id(2)

    @pl.when(c == 0)
    def _():
        st_ref[...] = s0_ref[...]

    row = lax.broadcasted_iota(jnp.int32, (C, C), 0)
    col = lax.broadcasted_iota(jnp.int32, (C, C), 1)
    lower = row >= col
    strict = row > col
    for n in range(nb):
        q, k, v = q_ref[n], k_ref[n], v_ref[n]
        g_col, g_row, beta = gc_ref[n, 0], gr_ref[n, 0, 0], bc_ref[n, 0]
        s = st_ref[n, 0]
        gcum_col = jnp.sum(jnp.where(lower, g_row, 0.0), axis=1, keepdims=True)
        gcum_row = jnp.sum(jnp.where(row <= col, g_col, 0.0), axis=0, keepdims=True)
        decay = jnp.where(lower, jnp.exp(jnp.where(lower, gcum_col - gcum_row, 0.0)), 0.0)
        kb = k * beta
        a = jnp.where(strict, _mm_nt(kb, k) * decay, 0.0)
        t = _neumann_inverse(-a, C)
        w = _mm_hi(t, kb * jnp.exp(gcum_col))
        u = _mm_hi(t, v * beta)
        aqk = jnp.where(lower, _mm_nt(q, k) * decay, 0.0)
        v_new = u - _mm(w, s)
        o_ref[n] = _mm(q * jnp.exp(gcum_col), s) + _mm(aqk, v_new)
        g_last = gcum_col[C - 1:C, :]
        st_ref[n, 0] = s * jnp.exp(g_last) + _mm_tn(k * jnp.exp(g_last - gcum_col), v_new)


def _gdn_scan(q, k, v, g, beta, s0, C, nb):
    bn, t, _ = q.shape
    nc = t // C
    g_hc = jnp.transpose(g, (0, 2, 1)).reshape(bn, H_B, nc, C)
    g_col = g_hc.reshape(bn, H_B, t, 1)
    g_row = g_hc.reshape(bn, H_B, nc, 1, C)
    b_col = jnp.transpose(beta, (0, 2, 1)).reshape(bn, H_B, t, 1)
    vspec = pl.BlockSpec((nb, C, LANES), lambda i, h, c: (i, c, h))
    cspec = pl.BlockSpec((nb, 1, C, 1), lambda i, h, c: (i, h, c, 0))
    rspec = pl.BlockSpec((nb, 1, 1, 1, C), lambda i, h, c: (i, h, c, 0, 0))
    sspec = pl.BlockSpec((nb, 1, LANES, LANES), lambda i, h, c: (i, h, 0, 0))
    return pl.pallas_call(
        functools.partial(_gdn_chunk_kernel, C=C, nb=nb),
        grid=(bn // nb, H_B, nc),
        in_specs=[vspec, vspec, vspec, cspec, rspec, cspec, sspec],
        out_specs=[vspec, sspec],
        out_shape=[jax.ShapeDtypeStruct((bn, t, DB), F32), jax.ShapeDtypeStruct(s0.shape, F32)],
        compiler_params=_cparams("parallel", "parallel", "arbitrary"),
        name="gdn_scan",
    )(q, k, v, g_col, g_row, b_col, s0)


def _diff_finish(acc1, l1, acc2, l2, lam, g, lam_init):
    o = acc1 / l1 - lam * (acc2 / l2)
    return o * lax.rsqrt(jnp.mean(o * o, axis=-1, keepdims=True) + NORM_EPS) * g * (1.0 - lam_init)


def _attn_prompt_kernel(q_ref, k_ref, v_ref, lam_ref, g_ref, o_ref, m_ref, l_ref, acc_ref, *, tq, tk, lam_init):
    qi, ki = pl.program_id(2), pl.program_id(3)

    @pl.when(ki == 0)
    def _():
        m_ref[...] = jnp.full_like(m_ref, -jnp.inf)
        l_ref[...] = jnp.zeros_like(l_ref)
        acc_ref[...] = jnp.zeros_like(acc_ref)

    @pl.when(ki * tk <= qi * tq + (tq - 1))
    def _():
        q = q_ref[0]
        kb = k_ref[0].astype(BF16)
        vb = v_ref[0].astype(BF16)
        first = lax.broadcasted_iota(jnp.int32, (tq, LANES), 1) < D_C
        q_pos = qi * tq + lax.broadcasted_iota(jnp.int32, (tq, tk), 0)
        k_pos = ki * tk + lax.broadcasted_iota(jnp.int32, (tq, tk), 1)
        visible = k_pos <= q_pos
        for comp in range(2):
            qc = jnp.where(first if comp == 0 else jnp.logical_not(first), q, 0.0)
            s = _mm_nt(qc, kb) * (D_C ** -0.5)
            s = jnp.where(visible, s, -jnp.inf)
            m_prev = m_ref[comp]
            m_next = jnp.maximum(m_prev, jnp.max(s, axis=1, keepdims=True))
            p = jnp.exp(s - m_next[:, 0:1])
            alpha = jnp.exp(m_prev - m_next)
            l_ref[comp] = alpha * l_ref[comp] + jnp.sum(p, axis=1, keepdims=True)
            acc_ref[comp] = alpha * acc_ref[comp] + jnp.dot(p.astype(BF16), vb, preferred_element_type=F32)
            m_ref[comp] = m_next

    @pl.when(ki == pl.num_programs(3) - 1)
    def _():
        o_ref[0] = _diff_finish(acc_ref[0], l_ref[0], acc_ref[1], l_ref[1], lam_ref[...], g_ref[...], lam_init)


def _attn_prompt(qkv3, lam, norm_g, lam_init, tq, tk):
    bn, t, _ = qkv3.shape

    def kv_block(qi, ki):
        return jnp.minimum(ki, (qi * tq + tq - 1) // tk)

    return pl.pallas_call(
        functools.partial(_attn_prompt_kernel, tq=tq, tk=tk, lam_init=lam_init),
        grid=(bn, H_C, t // tq, t // tk),
        in_specs=[pl.BlockSpec((1, tq, LANES), lambda b, h, qi, ki: (b, qi, h)),
                  pl.BlockSpec((1, tk, LANES), lambda b, h, qi, ki: (b, kv_block(qi, ki), H_C + h)),
                  pl.BlockSpec((1, tk, LANES), lambda b, h, qi, ki: (b, kv_block(qi, ki), 2 * H_C + h)),
                  pl.BlockSpec((1, LANES), lambda b, h, qi, ki: (0, 0)),
                  pl.BlockSpec((1, LANES), lambda b, h, qi, ki: (0, 0))],
        out_specs=pl.BlockSpec((1, tq, LANES), lambda b, h, qi, ki: (b, qi, h)),
        out_shape=jax.ShapeDtypeStruct((bn, t, D_MODEL), F32),
        scratch_shapes=[pltpu.VMEM((2, tq, LANES), F32), pltpu.VMEM((2, tq, LANES), F32),
                        pltpu.VMEM((2, tq, LANES), F32)],
        compiler_params=_cparams("parallel", "parallel", "parallel", "arbitrary"),
        name="diff_attn_prompt",
    )(qkv3, qkv3, qkv3, lam, norm_g.reshape(1, LANES))


def _attn_decode_kernel(pt_ref, q_ref, kn_ref, vn_ref, kc_ref, vc_ref, lam_ref, g_ref, o_ref, m_ref, l_ref, acc_ref,
                        *, lam_init):
    p_idx = pl.program_id(1)
    rows = 2 * H_C

    @pl.when(p_idx == 0)
    def _():
        m_ref[...] = jnp.full_like(m_ref, -jnp.inf)
        l_ref[...] = jnp.zeros_like(l_ref)
        acc_ref[...] = jnp.zeros_like(acc_ref)

    q = q_ref[0]
    first = lax.broadcasted_iota(jnp.int32, (H_C, LANES), 1) < D_C
    q2 = jnp.concatenate([jnp.where(first, q, 0.0), jnp.where(first, 0.0, q)], axis=0)

    n = PAGE_SIZE * H_C
    keys = kc_ref[0].reshape(n, LANES)
    vals = vc_ref[0].reshape(n, LANES)
    s = _mm_nt(q2, keys) * (D_C ** -0.5)
    same = ((lax.broadcasted_iota(jnp.int32, (rows, n), 0) % H_C)
            == (lax.broadcasted_iota(jnp.int32, (rows, n), 1) % H_C))
    s = jnp.where(same, s, -jnp.inf)
    m_prev = m_ref[...]
    m_next = jnp.maximum(m_prev, jnp.max(s, axis=1, keepdims=True))
    p = jnp.exp(s - m_next[:, 0:1])
    alpha = jnp.exp(m_prev - m_next)
    l_ref[...] = alpha * l_ref[...] + jnp.sum(p, axis=1, keepdims=True)
    acc_ref[...] = alpha * acc_ref[...] + _mm(p, vals)
    m_ref[...] = m_next

    @pl.when(p_idx == pl.num_programs(1) - 1)
    def _():
        k_self = jnp.concatenate([kn_ref[0], kn_ref[0]], axis=0)
        v_self = jnp.concatenate([vn_ref[0], vn_ref[0]], axis=0)
        s_self = jnp.sum(q2 * k_self, axis=1, keepdims=True) * (D_C ** -0.5)
        m_old = m_ref[...]
        m_fin = jnp.maximum(m_old, s_self)
        p_self = jnp.exp(s_self - m_fin)
        scale = jnp.exp(m_old - m_fin)
        l = scale * l_ref[...] + p_self
        acc = scale * acc_ref[...] + p_self * v_self
        o_ref[0] = _diff_finish(acc[:H_C], l[:H_C], acc[H_C:], l[H_C:], lam_ref[...], g_ref[...], lam_init)


def _attn_decode(qkv, cache_k, cache_v, layer, page_table, lam, norm_g, lam_init):
    bn = qkv.shape[0]
    n_pool = cache_k.shape[1]
    n_pages = page_table.shape[1]
    qkv4 = qkv.reshape(bn, 3 * H_C, LANES)
    kc = cache_k.reshape((-1,) + cache_k.shape[2:])
    vc = cache_v.reshape((-1,) + cache_v.shape[2:])
    pages = (page_table + layer * n_pool).reshape(-1)
    page_spec = pl.BlockSpec((1, PAGE_SIZE, H_C, LANES), lambda b, p, pt: (pt[b * n_pages + p], 0, 0, 0))
    vec = pl.BlockSpec((1, LANES), lambda b, p, pt: (0, 0))
    out = pl.pallas_call(
        functools.partial(_attn_decode_kernel, lam_init=lam_init),
        grid_spec=pltpu.PrefetchScalarGridSpec(
            num_scalar_prefetch=1, grid=(bn, n_pages),
            in_specs=[pl.BlockSpec((1, H_C, LANES), lambda b, p, pt: (b, 0, 0)),
                      pl.BlockSpec((1, H_C, LANES), lambda b, p, pt: (b, 1, 0)),
                      pl.BlockSpec((1, H_C, LANES), lambda b, p, pt: (b, 2, 0)),
                      page_spec, page_spec, vec, vec],
            out_specs=pl.BlockSpec((1, H_C, LANES), lambda b, p, pt: (b, 0, 0)),
            scratch_shapes=[pltpu.VMEM((2 * H_C, LANES), F32), pltpu.VMEM((2 * H_C, LANES), F32),
                            pltpu.VMEM((2 * H_C, LANES), F32)]),
        out_shape=jax.ShapeDtypeStruct((bn, H_C, LANES), F32),
        compiler_params=_cparams("parallel", "arbitrary"),
        name="diff_attn_decode",
    )(pages, qkv4, qkv4, qkv4, kc, vc, lam, norm_g.reshape(1, LANES))
    return out.reshape(bn, D_MODEL)


def _rope_tables(pos):
    half = ROT_DIM // 2
    inv = ROPE_THETA ** (-jnp.arange(half, dtype=F32) * 2.0 / ROT_DIM)
    ang = pos.astype(F32)[:, None] * inv[None, :]
    cos, sin = jnp.cos(ang), jnp.sin(ang)
    n = ang.shape[0]
    rest = D_C - ROT_DIM
    c64 = jnp.concatenate([cos, cos, jnp.ones((n, rest), F32)], axis=1)
    up64 = jnp.concatenate([jnp.zeros((n, half), F32), sin, jnp.zeros((n, rest), F32)], axis=1)
    dn64 = jnp.concatenate([-sin, jnp.zeros((n, half + rest), F32)], axis=1)
    return [jnp.tile(t, (1, LANES // D_C)) for t in (c64, up64, dn64)]


def _prep_weights(p):
    pad_ff = D_FF_PAD - D_FF
    w_in = p['ffn_w_in']
    wp = {
        'ffn_a': jnp.pad(w_in[..., :D_FF], ((0, 0), (0, 0), (0, 0), (0, pad_ff))).astype(BF16),
        'ffn_b': jnp.pad(w_in[..., D_FF:], ((0, 0), (0, 0), (0, 0), (0, pad_ff))).astype(BF16),
        'ffn_o': jnp.pad(p['ffn_w_out'], ((0, 0), (0, 0), (0, pad_ff), (0, 0))).astype(BF16),
        'mix_in_a': p['mix_w_in'][..., :A_COLS].astype(BF16),
        'mix_in_b': jnp.pad(p['mix_w_in'][..., A_COLS:], ((0, 0), (0, 0), (0, B_COLS_PAD - B_COLS))).astype(BF16),
        'mix_out_a': p['mix_w_out'][:, :DA].astype(BF16),
        'mix_out_b': p['mix_w_out'][:, DA:].astype(BF16),
        'att_in': p['att_w_in'].astype(BF16),
        'att_out': p['att_w_out'].astype(BF16),
    }
    n_mix = p['a_w2'].shape[0]
    z = lambda r: jnp.zeros((n_mix, r, DA), F32)
    wp['a_w2p'] = jnp.concatenate([p['a_w2'], z(LORA_A + LORA_G)], axis=1).astype(BF16)
    wp['a_a2p'] = jnp.concatenate([z(LORA_W), p['a_a2'], z(LORA_G)], axis=1).astype(BF16)
    wp['a_g2p'] = jnp.concatenate([z(LORA_W + LORA_A), p['a_g2']], axis=1).astype(BF16)
    head = jnp.arange(DA) // HD_A
    wp['gsum'] = (head[:, None] == head[None, :]).astype(F32)
    gpad = lambda t: jnp.pad(t, ((0, 0), (H_B, LANES - 2 * H_B)))[:, None, :]
    wp['b_alog'] = gpad(p['b_a_log'])
    wp['b_dtb'] = gpad(p['b_dt_bias'])
    lam = (jnp.exp(jnp.sum(p['lam_q1'] * p['lam_k1'], axis=-1)) - jnp.exp(jnp.sum(p['lam_q2'] * p['lam_k2'], axis=-1)))
    wp['lam_raw'] = lam
    return wp


def _mixer_ab(grp, h_args, p, wp, m, shift0, wkv0, conv0, ssm0, cfg):
    x, g, ks, kc = h_args
    bn, t = grp.n_seq, grp.seq_len
    proj_a = _norm_mod_matmul(grp, x, g, ks, kc, wp['mix_in_a'][m], cfg['tm_proj'], A_COLS)
    proj_b = _norm_mod_matmul(grp, x, g, ks, kc, wp['mix_in_b'][m], cfg['tm_proj'], B_COLS_PAD)
    prep = _rwkv_prep(grp, proj_a, shift0, _rwkv_params(p, wp, m), cfg['tm_elem'])
    r, lw, k2, v, na, b, bonus, gate = prep
    c_a, pad_a = cfg['chunk_a'], cfg['pad_t']
    vecs = [u.reshape(bn, t, DA) for u in (r, lw, k2, v, na, b)]
    if pad_a:
        vecs = [jnp.pad(u, ((0, 0), (0, pad_a), (0, 0))) for u in vecs]
    ya, wkv_d = _rwkv_scan(vecs, _pair_diag(wkv0), c_a, cfg['nb_scan'])
    wkv = _unpair_diag(wkv_d)
    ya = ya[:, :t].reshape(grp.rows, DA)
    ya = _rwkv_post(grp, ya, bonus, gate, p['a_gn_w'][m], p['a_gn_b'][m], wp['gsum'], cfg['tm_elem'])
    qb, kb, vb, gates = _gdn_prep(grp, proj_b, conv0, p['b_conv_w'][m], wp['b_alog'][m], wp['b_dtb'][m],
                                  cfg['tm_elem'])
    beta = gates[:, :H_B].reshape(bn, t, H_B)
    gdec = gates[:, H_B:2 * H_B].reshape(bn, t, H_B)
    seqs = [u.reshape(bn, t, -1) for u in (qb, kb, vb)] + [gdec, beta]
    if pad_a:
        seqs = [jnp.pad(u, ((0, 0), (0, pad_a), (0, 0))) for u in seqs]
    ob, ssm = _gdn_scan(*seqs, ssm0, cfg['chunk_b'], cfg['nb_scan'])
    ob = ob[:, :t].reshape(grp.rows, DB)
    yb = _gdn_post(grp, ob, proj_b, p['b_norm_g'][m], cfg['tm_elem'])
    pa3 = proj_a.reshape(bn, t, A_COLS)
    qkv_pre = proj_b.reshape(bn, t, B_COLS_PAD)[:, :, :B_CONV_CH]
    new_buf = jnp.concatenate([conv0.astype(F32), qkv_pre[:, max(t - (CONV_W - 1), 0):]], axis=1)[:, -(CONV_W - 1):]
    return (ya, yb), pa3[:, -1], wkv, new_buf, ssm


def _rwkv_params(p, wp, m):
    row = lambda v: v.reshape(1, -1)
    return [row(p['a_mu'][m]), row(p['a_w0'][m]), wp['a_w2p'][m], row(p['a_a0'][m]), wp['a_a2p'][m],
            wp['a_g2p'][m], row(p['a_kk'][m]), row(p['a_ka'][m]), row(p['a_rk'][m]), wp['gsum']]


def _run_trunk(bn, t, mod, x, pos, p, wp, shift0, wkv0, conv0, ssm0, caches, cfg):
    rope = _rope_tables(pos)
    shifts, wkvs, convs, ssms, ks, vs = [], [], [], [], [], []
    for l in range(DEPTH):
        grp_l = _Group(bn, t, mod[l])
        x = _ffn(grp_l, x, p['norm_g'][l, 0], 0, wp['ffn_a'][l, 0], wp['ffn_b'][l, 0], wp['ffn_o'][l, 0],
                 cfg['tm_ffn'], cfg['tf'])
        if l % 2 == 0:
            m = l // 2
            (ya, yb), s_sh, s_wkv, s_conv, s_ssm = _mixer_ab(
                grp_l, (x, p['norm_g'][l, 1], 3, 4), p, wp, m, shift0[m], wkv0[m], conv0[m], ssm0[m], cfg)
            shifts.append(s_sh)
            wkvs.append(s_wkv)
            convs.append(s_conv)
            ssms.append(s_ssm)
            x = _out_proj(grp_l, [ya, yb], [wp['mix_out_a'][m], wp['mix_out_b'][m]], x, 5, cfg['tm_proj'])
        else:
            ai = l // 2
            lam_init = 0.8 - 0.6 * math.exp(-0.3 * l)
            lam = jnp.full((1, LANES), wp['lam_raw'][ai] + lam_init, F32)
            qkv = _norm_mod_matmul(grp_l, x, p['norm_g'][l, 1], 3, 4, wp['att_in'][ai], cfg['tm_proj'], 512,
                                   rope=rope)
            if caches is None:
                o = _attn_prompt(qkv.reshape(bn, t, 3 * D_MODEL), lam, p['att_norm_g'][ai], lam_init,
                                 cfg['tq'], cfg['tk']).reshape(bn * t, D_MODEL)
            else:
                cache_k, cache_v, page_table = caches
                o = _attn_decode(qkv, cache_k, cache_v, ai, page_table, lam, p['att_norm_g'][ai], lam_init)
            ks.append(qkv[:, D_MODEL:2 * D_MODEL].reshape(bn, t, H_C, 2 * D_C))
            vs.append(qkv[:, 2 * D_MODEL:].reshape(bn, t, H_C, 2 * D_C))
            x = _out_proj(grp_l, [o], [wp['att_out'][ai]], x, 5, cfg['tm_proj'])
        x = _ffn(grp_l, x, p['norm_g'][l, 2], 6, wp['ffn_a'][l, 1], wp['ffn_b'][l, 1], wp['ffn_o'][l, 1],
                 cfg['tm_ffn'], cfg['tf'])
    y = _final_norm(grp_l, x, p['final_g'], cfg['tm_proj']).reshape(bn, t, D_MODEL)
    return (y, jnp.stack(shifts), jnp.stack(wkvs), jnp.stack(convs), jnp.stack(ssms), jnp.stack(ks), jnp.stack(vs))


PROMPT_CFG = dict(tm_ffn=1024, tf=256, tm_proj=512, tm_elem=256, chunk_a=64, chunk_b=64, pad_t=0, nb_scan=4,
                  tq=512, tk=512)
SAMPLE_CFG = dict(tm_ffn=128, tf=256, tm_proj=128, tm_elem=128, chunk_a=8, chunk_b=8, pad_t=7, nb_scan=8,
                  tq=0, tk=0)


def kernel(x_prompt, x_sample, state_a_shift, state_a_wkv, state_b_conv, state_b_ssm, cache_k, cache_v, page_table, c_prompt, c_sample, ada_w, ada_b, norm_g, ffn_w_in, ffn_w_out, mix_w_in, mix_w_out, a_mu, a_w0, a_w2, a_a0, a_a2, a_g2, a_kk, a_ka, a_rk, a_gn_w, a_gn_b, b_conv_w, b_a_log, b_dt_bias, b_norm_g, att_w_in, att_w_out, lam_q1, lam_k1, lam_q2, lam_k2, att_norm_g, final_g):
    p = dict(ada_w=ada_w, ada_b=ada_b, norm_g=norm_g, ffn_w_in=ffn_w_in, ffn_w_out=ffn_w_out,
             mix_w_in=mix_w_in, mix_w_out=mix_w_out, a_mu=a_mu, a_w0=a_w0, a_w2=a_w2, a_a0=a_a0, a_a2=a_a2,
             a_g2=a_g2, a_kk=a_kk, a_ka=a_ka, a_rk=a_rk, a_gn_w=a_gn_w, a_gn_b=a_gn_b, b_conv_w=b_conv_w,
             b_a_log=b_a_log, b_dt_bias=b_dt_bias, b_norm_g=b_norm_g, att_w_in=att_w_in, att_w_out=att_w_out,
             lam_q1=lam_q1, lam_k1=lam_k1, lam_q2=lam_q2, lam_k2=lam_k2, att_norm_g=att_norm_g, final_g=final_g)
    bp, tp, _ = x_prompt.shape
    bs, ts, _ = x_sample.shape
    n_mix = state_a_shift.shape[0]
    wp = _prep_weights(p)
    rows = bp + bs
    rows_pad = -(-rows // 16) * 16
    c_all = jnp.pad(jnp.concatenate([c_prompt, c_sample], axis=0), ((0, rows_pad - rows), (0, 0)))
    mod = _ada_mod(c_all, ada_w, ada_b)
    out_p = _run_trunk(
        bp, tp, mod[:, :bp], x_prompt.reshape(bp * tp, D_MODEL), jnp.arange(tp), p, wp,
        jnp.zeros((n_mix, bp, A_COLS), F32), jnp.zeros((n_mix, bp, H_A, HD_A, HD_A), F32),
        jnp.zeros((n_mix, bp, CONV_W - 1, B_CONV_CH), F32), jnp.zeros((n_mix, bp, H_B, HD_B, HD_B), F32),
        None, PROMPT_CFG)
    past_len = page_table.shape[1] * PAGE_SIZE
    out_s = _run_trunk(
        bs, ts, mod[:, bp:bp + bs], x_sample.reshape(bs * ts, D_MODEL), past_len + jnp.arange(ts), p, wp,
        state_a_shift, state_a_wkv, state_b_conv, state_b_ssm, (cache_k, cache_v, page_table), SAMPLE_CFG)
    return (out_p[0], out_s[0]) + out_p[1:] + out_s[1:]
```
